```python
import math
import jax, jax.numpy as jnp
from jax import lax
import numpy as np

D_MODEL = 4096
BATCH = 1
SEQ = 8192
DEPTH = 2

GRID_W = 64
CTX_LEN = 256
EPS = 1e-6

CONV_DIM = 1024
CONV_WIDTH = 31
CONV_PAD = CONV_WIDTH // 2
MLA_HEADS = 8
Q_LORA = 1024
KV_LORA = 512
QK_NOPE = 128
QK_ROPE = 64
V_HEAD = 128
MLA_WIDTH = MLA_HEADS * V_HEAD
MLA_SCALE = (QK_NOPE + QK_ROPE) ** -0.5
ROPE_THETA = 10000.0
ROPE_PER_AXIS = QK_ROPE // 4
Q_BLOCK = 128
GMLP_HEADS = 8
GMLP_HEAD_DIM = 128
GMLP_WIDTH = GMLP_HEADS * GMLP_HEAD_DIM
CHUNK = 128
FOURIER_GROUPS = 4
FOURIER_DIM = 1024
FOURIER_GROUP_DIM = FOURIER_DIM // FOURIER_GROUPS
MIX_WIDTH = MLA_WIDTH + CONV_DIM + GMLP_WIDTH + FOURIER_DIM
KV_COLS = KV_LORA + QK_ROPE
OFF_Q = KV_COLS
OFF_CONV = OFF_Q + Q_LORA
OFF_GMLP = OFF_CONV + 2 * CONV_DIM
OFF_FOUR = OFF_GMLP + 2 * GMLP_WIDTH
IN_COLS = OFF_FOUR + FOURIER_DIM
N_EXPERTS = 32
TOP_K = 4
D_FF = 512
SWIGLU_LIMIT = 7.0
SWIGLU_ALPHA = 1.702
N_MOD = 6

kernel_name = "hybrid_dit_parallel_groups_mla_conv_gmlp_fnet_moe"


def rms_norm(x, g):
    xf = x.astype(jnp.float32)
    y = xf * lax.rsqrt(jnp.mean(xf * xf, axis=-1, keepdims=True) + EPS)
    return (y * g.astype(jnp.float32)).astype(x.dtype)


def modulate(h, shift, scale):
    return h * (1.0 + scale) + shift


def axial_rope(n_tokens, dtype):
    rows = n_tokens // GRID_W
    r, col = jnp.meshgrid(jnp.arange(rows, dtype=jnp.float32), jnp.arange(GRID_W, dtype=jnp.float32), indexing="ij")
    inv = ROPE_THETA ** (-jnp.arange(ROPE_PER_AXIS, dtype=jnp.float32) / ROPE_PER_AXIS)
    ang = jnp.concatenate([r.reshape(-1, 1) * inv, col.reshape(-1, 1) * inv], axis=-1)
    return jnp.cos(ang).astype(dtype), jnp.sin(ang).astype(dtype)


def apply_rope(x, cos, sin):
    half = x.shape[-1] // 2
    x1, x2 = x[..., :half], x[..., half:]
    return jnp.concatenate([x1 * cos - x2 * sin, x1 * sin + x2 * cos], axis=-1)


def mla_kv(p_kv, kv_norm_g, w_ukv):
    c_kv = rms_norm(p_kv[..., :KV_LORA], kv_norm_g)
    k_rope = p_kv[..., KV_LORA:]
    kv = jnp.einsum("bnc,che->bnhe", c_kv, w_ukv)
    return kv[..., :QK_NOPE], k_rope, kv[..., QK_NOPE:]


def mla_q(p_q, q_norm_g, w_uq):
    q = jnp.einsum("bnc,che->bnhe", rms_norm(p_q, q_norm_g), w_uq)
    return q[..., :QK_NOPE], q[..., QK_NOPE:]


def attend(q_nope, q_rope, k_nope, k_rope, v):
    s = jnp.einsum("bqhd,bkhd->bhqk", q_nope, k_nope) + jnp.einsum("bqhr,bkr->bhqk", q_rope, k_rope)
    p = jax.nn.softmax(s.astype(jnp.float32) * MLA_SCALE, axis=-1).astype(v.dtype)
    return jnp.einsum("bhqk,bkhd->bqhd", p, v)


def mla_mixer(p_lat, p_ctx, q_norm_g, w_uq, kv_norm_g, w_ukv, update_ctx):
    B, S, _ = p_lat.shape
    n_ctx = p_ctx.shape[1]
    cos, sin = axial_rope(S, p_lat.dtype)
    kn_c, kr_c, v_c = mla_kv(p_ctx[..., :KV_COLS], kv_norm_g, w_ukv)
    kn_l, kr_l, v_l = mla_kv(p_lat[..., :KV_COLS], kv_norm_g, w_ukv)
    kr_l = apply_rope(kr_l, cos[None], sin[None])
    qn_l, qr_l = mla_q(p_lat[..., OFF_Q:OFF_CONV], q_norm_g, w_uq)
    qr_l = apply_rope(qr_l, cos[None, :, None], sin[None, :, None])
    k_nope = jnp.concatenate([kn_c, kn_l], axis=1)
    k_rope = jnp.concatenate([kr_c, kr_l], axis=1)
    v = jnp.concatenate([v_c, v_l], axis=1)
    nb = S // Q_BLOCK
    qn_b = qn_l.reshape(B, nb, Q_BLOCK, MLA_HEADS, QK_NOPE).swapaxes(0, 1)
    qr_b = qr_l.reshape(B, nb, Q_BLOCK, MLA_HEADS, QK_ROPE).swapaxes(0, 1)
    o_blocks = lax.map(lambda qs: attend(qs[0], qs[1], k_nope, k_rope, v), (qn_b, qr_b))
    o_lat = o_blocks.swapaxes(0, 1).reshape(B, S, MLA_WIDTH)
    o_ctx = None
    if update_ctx:
        qn_c, qr_c = mla_q(p_ctx[..., OFF_Q:OFF_CONV], q_norm_g, w_uq)
        o_ctx = attend(qn_c, qr_c, kn_c, kr_c, v_c).reshape(B, n_ctx, MLA_WIDTH)
    return o_lat, o_ctx


def conv_module(p, conv_w, conv_b, conv_norm_g):
    a = p[..., :CONV_DIM] * jax.nn.sigmoid(p[..., CONV_DIM:])
    y = lax.conv_general_dilated(
        a, conv_w[:, None, :].astype(a.dtype), window_strides=(1,), padding=[(CONV_PAD, CONV_PAD)],
        dimension_numbers=("NWC", "WIO", "NWC"), feature_group_count=CONV_DIM) + conv_b
    return jax.nn.silu(rms_norm(y, conv_norm_g))


def spatial_gating(p, v_norm_g, w_s, b_s):
    B, N, _ = p.shape
    u = p[..., :GMLP_WIDTH]
    v = rms_norm(p[..., GMLP_WIDTH:].reshape(B, N, GMLP_HEADS, GMLP_HEAD_DIM), v_norm_g)
    v = v.reshape(B, N // CHUNK, CHUNK, GMLP_HEADS, GMLP_HEAD_DIM)
    sv = jnp.einsum("hpq,bnqhd->bnphd", w_s, v) + b_s.T[:, :, None]
    return u * sv.reshape(B, N, GMLP_WIDTH)


def fourier_mix(p):
    B, N, _ = p.shape
    z = p.astype(jnp.float32).reshape(B, N, FOURIER_GROUPS, FOURIER_GROUP_DIM)
    f = jnp.fft.fft2(z, axes=(1, 3), norm="ortho").real
    return f.reshape(B, N, FOURIER_DIM).astype(p.dtype)


def seq_mixers(p, conv_w, conv_b, conv_norm_g, v_norm_g, w_s, b_s):
    a = conv_module(p[..., OFF_CONV:OFF_GMLP], conv_w, conv_b, conv_norm_g)
    g = spatial_gating(p[..., OFF_GMLP:OFF_FOUR], v_norm_g, w_s, b_s)
    f = fourier_mix(p[..., OFF_FOUR:])
    return a, g, f


def moe(h, router_w, router_b, w1, b1, w2, b2):
    B, N, D = h.shape
    t = h.reshape(B * N, D)
    logits = (t @ router_w + router_b).astype(jnp.float32)
    top_v, top_i = lax.top_k(logits, TOP_K)
    wts = jax.nn.softmax(top_v, axis=-1)
    gates = jnp.einsum("tk,tke->te", wts, jax.nn.one_hot(top_i, N_EXPERTS, dtype=jnp.float32)).astype(t.dtype)
    y = jnp.zeros_like(t)
    for e in range(N_EXPERTS):
        z = t @ w1[e] + b1[e]
        glu = jnp.minimum(z[:, :D_FF], SWIGLU_LIMIT)
        lin = jnp.clip(z[:, D_FF:], -SWIGLU_LIMIT, SWIGLU_LIMIT)
        act = glu * jax.nn.sigmoid(SWIGLU_ALPHA * glu) * (lin + 1.0)
        y = y + gates[:, e:e + 1] * (act @ w2[e] + b2[e])
    return y.reshape(B, N, D)


def setup_inputs(seed: int = 0) -> dict:
    key = jax.random.key(seed)
    ks = jax.random.split(key, 32)
    f32 = jnp.float32

    def nrm(k, shape, scale):
        return jax.random.normal(k, shape, f32) * scale

    def gain(k, shape):
        return 1.0 + 0.05 * jax.random.normal(k, shape, f32)

    L = DEPTH
    return {
        "x": nrm(ks[0], (BATCH, SEQ, D_MODEL), 1.0),
        "c": nrm(ks[1], (BATCH, D_MODEL), 1.0),
        "ctx": nrm(ks[2], (BATCH, CTX_LEN, D_MODEL), 1.0),
        "c_ctx": nrm(ks[3], (D_MODEL,), 1.0),
        "norm1_g": gain(ks[4], (L, D_MODEL)),
        "norm2_g": gain(ks[5], (L, D_MODEL)),
        "w_ada": nrm(ks[6], (L, D_MODEL, N_MOD * D_MODEL), 0.5 * D_MODEL ** -0.5),
        "b_ada": nrm(ks[7], (L, N_MOD * D_MODEL), 0.02),
        "w_in": nrm(ks[8], (L, D_MODEL, IN_COLS), D_MODEL ** -0.5),
        "q_norm_g": gain(ks[9], (L, Q_LORA)),
        "w_uq": nrm(ks[10], (L, Q_LORA, MLA_HEADS, QK_NOPE + QK_ROPE), Q_LORA ** -0.5),
        "kv_norm_g": gain(ks[11], (L, KV_LORA)),
        "w_ukv": nrm(ks[12], (L, KV_LORA, MLA_HEADS, QK_NOPE + V_HEAD), KV_LORA ** -0.5),
        "conv_w": nrm(ks[13], (L, CONV_WIDTH, CONV_DIM), CONV_WIDTH ** -0.5),
        "conv_b": nrm(ks[14], (L, CONV_DIM), 0.02),
        "conv_norm_g": gain(ks[15], (L, CONV_DIM)),
        "gmlp_v_norm_g": gain(ks[16], (L, GMLP_HEADS, GMLP_HEAD_DIM)),
        "gmlp_ws": nrm(ks[17], (L, GMLP_HEADS, CHUNK, CHUNK), CHUNK ** -0.5),
        "gmlp_bs": 1.0 + nrm(ks[18], (L, GMLP_HEADS, CHUNK), 0.02),
        "w_out": nrm(ks[19], (L, MIX_WIDTH, D_MODEL), MIX_WIDTH ** -0.5),
        "router_w": nrm(ks[20], (L, D_MODEL, N_EXPERTS), D_MODEL ** -0.5),
        "router_b": nrm(ks[21], (L, N_EXPERTS), 0.01),
        "exp_w1": nrm(ks[22], (L, N_EXPERTS, D_MODEL, 2 * D_FF), D_MODEL ** -0.5),
        "exp_b1": nrm(ks[23], (L, N_EXPERTS, 2 * D_FF), 0.01),
        "exp_w2": nrm(ks[24], (L, N_EXPERTS, D_FF, D_MODEL), D_FF ** -0.5),
        "exp_b2": nrm(ks[25], (L, N_EXPERTS, D_MODEL), 0.01),
        "final_norm_g": gain(ks[26], (D_MODEL,)),
    }


def reference(x, c, ctx, c_ctx, norm1_g, norm2_g, w_ada, b_ada, w_in, q_norm_g, w_uq, kv_norm_g, w_ukv,
              conv_w, conv_b, conv_norm_g, gmlp_v_norm_g, gmlp_ws, gmlp_bs, w_out, router_w, router_b,
              exp_w1, exp_b1, exp_w2, exp_b2, final_norm_g):
    n_ctx = ctx.shape[1]
    for l in range(DEPTH):
        last = l == DEPTH - 1
        mod_lat = (jax.nn.silu(c) @ w_ada[l] + b_ada[l])[:, None, :]
        sh1, sc1, g1, sh2, sc2, g2 = jnp.split(mod_lat, N_MOD, axis=-1)
        n_mod_ctx = 2 if last else N_MOD
        mod_ctx = jax.nn.silu(c_ctx) @ w_ada[l][:, :n_mod_ctx * D_MODEL] + b_ada[l][:n_mod_ctx * D_MODEL]
        mod_ctx = jnp.split(mod_ctx, n_mod_ctx)

        h_lat = modulate(rms_norm(x, norm1_g[l]), sh1, sc1)
        h_ctx = modulate(rms_norm(ctx, norm1_g[l]), mod_ctx[0], mod_ctx[1])
        p_lat = h_lat @ w_in[l]
        p_ctx = h_ctx @ (w_in[l][:, :KV_COLS] if last else w_in[l])
        o_mla_lat, o_mla_ctx = mla_mixer(p_lat, p_ctx, q_norm_g[l], w_uq[l], kv_norm_g[l], w_ukv[l], not last)
        a_l, g_l, f_l = seq_mixers(p_lat, conv_w[l], conv_b[l], conv_norm_g[l], gmlp_v_norm_g[l], gmlp_ws[l], gmlp_bs[l])
        x = x + g1 * (jnp.concatenate([o_mla_lat, a_l, g_l, f_l], axis=-1) @ w_out[l])

        if not last:
            a_c, g_c, f_c = seq_mixers(p_ctx, conv_w[l], conv_b[l], conv_norm_g[l], gmlp_v_norm_g[l], gmlp_ws[l], gmlp_bs[l])
            ctx = ctx + mod_ctx[2] * (jnp.concatenate([o_mla_ctx, a_c, g_c, f_c], axis=-1) @ w_out[l])
            h2_lat = modulate(rms_norm(x, norm2_g[l]), sh2, sc2)
            h2_ctx = modulate(rms_norm(ctx, norm2_g[l]), mod_ctx[3], mod_ctx[4])
            y = moe(jnp.concatenate([h2_ctx, h2_lat], axis=1), router_w[l], router_b[l],
                    exp_w1[l], exp_b1[l], exp_w2[l], exp_b2[l])
            ctx = ctx + mod_ctx[5] * y[:, :n_ctx]
            x = x + g2 * y[:, n_ctx:]
        else:
            h2_lat = modulate(rms_norm(x, norm2_g[l]), sh2, sc2)
            x = x + g2 * moe(h2_lat, router_w[l], router_b[l], exp_w1[l], exp_b1[l], exp_w2[l], exp_b2[l])
    return rms_norm(x, final_norm_g)
```

```python
import functools
import math

import numpy as np
import jax
import jax.numpy as jnp
from jax import lax
from jax.experimental import pallas as pl
from jax.experimental.pallas import tpu as pltpu

f32 = jnp.float32
bf16 = jnp.bfloat16
i32 = jnp.int32

D = 4096
SEQ = 8192
NC = 256
N = NC + SEQ
DEPTH = 2
GRID_W = 64
EPS = 1e-6
CONV_W = 31
CONV_PAD = CONV_W // 2
HEADS = 8
Q_LORA = 1024
KV_LORA = 512
QK_NOPE = 128
QK_ROPE = 64
V_HEAD = 128
QK_PAD = 256
MLA_SCALE = (QK_NOPE + QK_ROPE) ** -0.5
ROPE_THETA = 10000.0
CHUNK = 128
FGROUPS = 4
FGD = 256
N_EXP = 32
TOP_K = 4
D_FF = 512
LIMIT = 7.0
ALPHA = 1.702
N_MOD = 6
LANES = 128

TM = 256
NT = N // TM
TM_MM = 768
P_COLS = 7168
C_CONV_A, C_CONV_G, C_GM_U, C_GM_V, C_FOUR, C_Q, C_KV = 0, 1024, 2048, 3072, 4096, 5120, 6144
N_SLOT = N * TOP_K
N_STEP = N_SLOT // TM + N_EXP
FN1, FN2 = 64, 128
VMEM_BIG = 56 * 1024 * 1024


def _cparams(sem, vmem=None):
    return pltpu.CompilerParams(dimension_semantics=sem, vmem_limit_bytes=vmem)


def _rms(x):
    return x * lax.rsqrt(jnp.mean(x * x, axis=-1, keepdims=True) + EPS)


RC = 16


def _for_rows(n, fn):
    def body(c, carry):
        fn(pl.multiple_of(c * RC, RC))
        return carry

    lax.fori_loop(0, n // RC, body, 0)


def _ada_kernel(cc_ref, w_ref, b_ref, o_ref):
    s = cc_ref[...]
    s = s * jax.nn.sigmoid(s)
    hi = s.astype(bf16).astype(f32)
    row = lax.broadcasted_iota(i32, s.shape, 0)
    lhs = jnp.where(row < 2, hi, s - hi).astype(bf16)
    acc = jnp.dot(lhs, w_ref[...].astype(bf16), preferred_element_type=f32)
    orow = lax.broadcasted_iota(i32, acc.shape, 0)
    o_ref[...] = acc + jnp.where(orow < 2, b_ref[...], 0.0)


def _ada(cc, w_ada, b_ada, l):
    tn = 1024
    out = pl.pallas_call(
        _ada_kernel,
        grid=(N_MOD * D // tn,),
        in_specs=[pl.BlockSpec((8, D), lambda j: (0, 0)),
                  pl.BlockSpec((None, D, tn), lambda j: (l, 0, j)),
                  pl.BlockSpec((None, 1, tn), lambda j: (l, 0, j))],
        out_specs=pl.BlockSpec((8, tn), lambda j: (0, j)),
        out_shape=jax.ShapeDtypeStruct((8, N_MOD * D), f32),
        compiler_params=_cparams(("arbitrary",), VMEM_BIG),
        name="ada",
    )(cc, w_ada, b_ada.reshape(DEPTH, 1, N_MOD * D))
    return out[0:2] + out[2:4]


def _inproj_kernel(x_ref, g_ref, sh_ref, sc_ref, w_ref, o_ref, h_ref):
    m = pl.program_id(0)

    @pl.when(pl.program_id(1) == 0)
    def _():
        def chunk(r0):
            rows = pl.ds(r0, RC)
            xn = _rms(x_ref[rows, :]) * g_ref[...]
            is_ctx = m * TM_MM + r0 < NC
            sc = jnp.where(is_ctx, sc_ref[1:2, :], sc_ref[0:1, :])
            sh = jnp.where(is_ctx, sh_ref[1:2, :], sh_ref[0:1, :])
            h_ref[rows, :] = (xn * (1.0 + sc) + sh).astype(bf16)

        _for_rows(TM_MM, chunk)

    o_ref[...] = jnp.dot(h_ref[...], w_ref[...], preferred_element_type=f32).astype(bf16)


def _inproj(x, g, mod, w):
    tn = 512
    return pl.pallas_call(
        _inproj_kernel,
        grid=(N // TM_MM, P_COLS // tn),
        in_specs=[pl.BlockSpec((TM_MM, D), lambda m, n: (m, 0)),
                  pl.BlockSpec((1, D), lambda m, n: (0, 0)),
                  pl.BlockSpec((2, D), lambda m, n: (0, 0)),
                  pl.BlockSpec((2, D), lambda m, n: (0, 1)),
                  pl.BlockSpec((D, tn), lambda m, n: (0, n))],
        out_specs=pl.BlockSpec((TM_MM, tn), lambda m, n: (m, n)),
        out_shape=jax.ShapeDtypeStruct((N, P_COLS), bf16),
        scratch_shapes=[pltpu.VMEM((TM_MM, D), bf16)],
        compiler_params=_cparams(("arbitrary", "arbitrary"), VMEM_BIG),
        name="inproj",
    )(x, g, mod, mod, w)


def _mla_prep_kernel(pq_ref, pkv_ref, wq_ref, wkn_ref, wv_ref, qg_ref, kvg_ref,
                     cos_ref, sa_ref, sb_ref, q_out, k_out, v_out):
    cos = cos_ref[...]
    sa = sa_ref[...]
    sb = sb_ref[...]

    def rope(v):
        return v * cos + pltpu.roll(v, 96, 1) * sa + pltpu.roll(v, 32, 1) * sb

    a = (_rms(pq_ref[...].astype(f32)) * qg_ref[...]).astype(bf16)
    q = jnp.dot(a, wq_ref[...], preferred_element_type=f32)
    for h in range(HEADS):
        b = h * QK_PAD
        q_out[h, :, 0:QK_NOPE] = (q[:, b:b + QK_NOPE] * MLA_SCALE).astype(bf16)
        q_out[h, :, QK_NOPE:QK_PAD] = (rope(q[:, b + QK_NOPE:b + QK_PAD]) * MLA_SCALE).astype(bf16)

    pkv = pkv_ref[...].astype(f32)
    c = (_rms(pkv[:, :KV_LORA]) * kvg_ref[...]).astype(bf16)
    kn = jnp.dot(c, wkn_ref[...], preferred_element_type=f32)
    vv = jnp.dot(c, wv_ref[...], preferred_element_type=f32)
    kr = rope(pkv[:, KV_LORA:KV_LORA + LANES]).astype(bf16)
    for h in range(HEADS):
        k_out[h, :, 0:QK_NOPE] = kn[:, h * QK_NOPE:(h + 1) * QK_NOPE].astype(bf16)
        k_out[h, :, QK_NOPE:QK_PAD] = kr
        v_out[h] = vv[:, h * V_HEAD:(h + 1) * V_HEAD].astype(bf16)


def _mla_prep(p, wq, wkn, wv, qg, kvg, cosf, sina, sinb):
    const = lambda i: (0, 0)
    return pl.pallas_call(
        _mla_prep_kernel,
        grid=(NT,),
        in_specs=[pl.BlockSpec((TM, 1024), lambda i: (i, C_Q // 1024)),
                  pl.BlockSpec((TM, 1024), lambda i: (i, C_KV // 1024)),
                  pl.BlockSpec((Q_LORA, HEADS * QK_PAD), const),
                  pl.BlockSpec((KV_LORA, HEADS * QK_NOPE), const),
                  pl.BlockSpec((KV_LORA, HEADS * V_HEAD), const),
                  pl.BlockSpec((1, Q_LORA), const),
                  pl.BlockSpec((1, KV_LORA), const),
                  pl.BlockSpec((TM, LANES), lambda i: (i, 0)),
                  pl.BlockSpec((TM, LANES), lambda i: (i, 0)),
                  pl.BlockSpec((TM, LANES), lambda i: (i, 0))],
        out_specs=[pl.BlockSpec((HEADS, TM, QK_PAD), lambda i: (0, i, 0)),
                   pl.BlockSpec((HEADS, TM, QK_PAD), lambda i: (0, i, 0)),
                   pl.BlockSpec((HEADS, TM, V_HEAD), lambda i: (0, i, 0))],
        out_shape=[jax.ShapeDtypeStruct((HEADS, N, QK_PAD), bf16),
                   jax.ShapeDtypeStruct((HEADS, N, QK_PAD), bf16),
                   jax.ShapeDtypeStruct((HEADS, N, V_HEAD), bf16)],
        compiler_params=_cparams(("arbitrary",), VMEM_BIG),
        name="mla_prep",
    )(p, p, wq, wkn, wv, qg, kvg, cosf, sina, sinb)


def _attn_kernel(q_ref, k_ref, v_ref, o_ref, *, tk):
    qi = pl.program_id(1)
    q = q_ref[0]
    nblk = jnp.where(qi == 0, NC // tk, N // tk)

    def body(j, carry):
        m, l, acc = carry
        start = pl.multiple_of(j * tk, tk)
        k = k_ref[0, pl.ds(start, tk), :]
        v = v_ref[0, pl.ds(start, tk), :]
        s = lax.dot_general(q, k, (((1,), (1,)), ((), ())), preferred_element_type=f32)
        m_new = jnp.maximum(m, jnp.max(s, axis=-1, keepdims=True))
        alpha = jnp.exp(m - m_new)
        p = jnp.exp(s - m_new)
        l = alpha * l + jnp.sum(p, axis=-1, keepdims=True)
        acc = alpha * acc + jnp.dot(p.astype(bf16), v, preferred_element_type=f32)
        return m_new, l, acc

    m0 = jnp.full((TM, 1), -jnp.inf, f32)
    l0 = jnp.zeros((TM, 1), f32)
    a0 = jnp.zeros((TM, V_HEAD), f32)
    _, l, acc = lax.fori_loop(0, nblk, body, (m0, l0, a0))
    o_ref[...] = (acc / l).astype(bf16)


def _attention(q, k, v):
    return pl.pallas_call(
        functools.partial(_attn_kernel, tk=256),
        grid=(HEADS, NT),
        in_specs=[pl.BlockSpec((1, TM, QK_PAD), lambda h, i: (h, i, 0)),
                  pl.BlockSpec((1, N, QK_PAD), lambda h, i: (h, 0, 0)),
                  pl.BlockSpec((1, N, V_HEAD), lambda h, i: (h, 0, 0))],
        out_specs=pl.BlockSpec((TM, V_HEAD), lambda h, i: (i, h)),
        out_shape=jax.ShapeDtypeStruct((N, HEADS * V_HEAD), bf16),
        compiler_params=_cparams(("arbitrary", "arbitrary"), VMEM_BIG),
        name="attn",
    )(q, k, v)


_HALO = 16
_CR = 32


def _conv_kernel(ac, gc, ap, gp, an, gn, w_ref, b_ref, ng_ref, o_ref, buf, ybuf):
    i = pl.program_id(0)
    last = pl.num_programs(0) - 1

    def glu(a, g):
        return a[...].astype(f32) * jax.nn.sigmoid(g[...].astype(f32))

    has_prev = i >= 2
    has_next = jnp.logical_and(i >= 1, i < last)
    buf[0:_HALO, :] = jnp.where(has_prev, glu(ap, gp), 0.0)
    buf[_HALO:_HALO + TM, :] = glu(ac, gc)
    buf[_HALO + TM:2 * _HALO + TM, :] = jnp.where(has_next, glu(an, gn), 0.0)

    for cb in range(1024 // LANES):
        cols = slice(cb * LANES, (cb + 1) * LANES)
        wcol = w_ref[:, cols]
        bias = b_ref[:, cols]
        for rb in range(TM // _CR):
            acc = jnp.zeros((_CR, LANES), f32) + bias
            for j in range(CONV_W):
                r0 = rb * _CR + j + (_HALO - CONV_PAD)
                acc = acc + wcol[j:j + 1, :] * buf[r0:r0 + _CR, cols]
            ybuf[rb * _CR:(rb + 1) * _CR, cols] = acc

    y = _rms(ybuf[...]) * ng_ref[...]
    o_ref[...] = (y * jax.nn.sigmoid(y)).astype(bf16)


def _conv(p, w, b, ng):
    nb = TM // _HALO
    const = lambda i: (0, 0)
    prev = lambda c: (lambda i: (jnp.maximum(i * nb - 1, 0), c))
    nxt = lambda c: (lambda i: (jnp.minimum((i + 1) * nb, N // _HALO - 1), c))
    return pl.pallas_call(
        _conv_kernel,
        grid=(NT,),
        in_specs=[pl.BlockSpec((TM, 1024), lambda i: (i, C_CONV_A // 1024)),
                  pl.BlockSpec((TM, 1024), lambda i: (i, C_CONV_G // 1024)),
                  pl.BlockSpec((_HALO, 1024), prev(C_CONV_A // 1024)),
                  pl.BlockSpec((_HALO, 1024), prev(C_CONV_G // 1024)),
                  pl.BlockSpec((_HALO, 1024), nxt(C_CONV_A // 1024)),
                  pl.BlockSpec((_HALO, 1024), nxt(C_CONV_G // 1024)),
                  pl.BlockSpec((32, 1024), const),
                  pl.BlockSpec((1, 1024), const),
                  pl.BlockSpec((1, 1024), const)],
        out_specs=pl.BlockSpec((TM, 1024), lambda i: (i, 0)),
        out_shape=jax.ShapeDtypeStruct((N, 1024), bf16),
        scratch_shapes=[pltpu.VMEM((TM + 2 * _HALO, 1024), f32), pltpu.VMEM((TM, 1024), f32)],
        compiler_params=_cparams(("arbitrary",)),
        name="conv",
    )(p, p, p, p, p, p, w, b, ng)


def _gmlp_kernel(u_ref, v_ref, ws_ref, bs_ref, vg_ref, o_ref):
    for ck in range(TM // CHUNK):
        rows = slice(ck * CHUNK, (ck + 1) * CHUNK)
        for h in range(HEADS):
            cols = slice(h * LANES, (h + 1) * LANES)
            vn = (_rms(v_ref[rows, cols].astype(f32)) * vg_ref[h:h + 1, :]).astype(bf16)
            sv = jnp.dot(ws_ref[h], vn, preferred_element_type=f32) + bs_ref[h]
            o_ref[rows, cols] = (u_ref[rows, cols].astype(f32) * sv).astype(bf16)


def _gmlp(p, ws, bs, vg):
    return pl.pallas_call(
        _gmlp_kernel,
        grid=(NT,),
        in_specs=[pl.BlockSpec((TM, 1024), lambda i: (i, C_GM_U // 1024)),
                  pl.BlockSpec((TM, 1024), lambda i: (i, C_GM_V // 1024)),
                  pl.BlockSpec((HEADS, CHUNK, CHUNK), lambda i: (0, 0, 0)),
                  pl.BlockSpec((HEADS, CHUNK, LANES), lambda i: (0, 0, 0)),
                  pl.BlockSpec((HEADS, LANES), lambda i: (0, 0))],
        out_specs=pl.BlockSpec((TM, 1024), lambda i: (i, 0)),
        out_shape=jax.ShapeDtypeStruct((N, 1024), bf16),
        compiler_params=_cparams(("arbitrary",)),
        name="gmlp",
    )(p, p, ws, bs, vg)


def _cs(rows, cols, period):
    k = (np.arange(rows, dtype=np.int64)[:, None] * np.arange(cols, dtype=np.int64)[None, :]) % period
    ang = 2.0 * np.pi * k.astype(np.float64) / period
    return np.cos(ang).astype(np.float32), np.sin(ang).astype(np.float32)


def _fourier_tables():
    c256, s256 = _cs(FGD, FGD, FGD)
    c1, s1 = _cs(FN1, FN1, FN1)
    c2, s2 = _cs(FN2, FN2, FN2)
    ct, st = _cs(FN1, FN2, SEQ)
    return dict(
        tch=jnp.asarray(np.concatenate([c256, s256], axis=1)).astype(bf16),
        g1=jnp.asarray(np.block([[c1, -s1], [-s1, -c1]])).astype(bf16),
        t2=jnp.asarray(np.concatenate([c2, s2], axis=1)).astype(bf16),
        ct=jnp.asarray(ct)[:, :, None],
        st=jnp.asarray(st)[:, :, None],
        tctx=jnp.asarray(np.concatenate([c256, -s256], axis=1)).astype(bf16),
    )


def _four_ch_kernel(z_ref, t_ref, a_ref, b_ref):
    for g in range(FGROUPS):
        cols = slice(g * FGD, (g + 1) * FGD)
        r = jnp.dot(z_ref[:, cols], t_ref[...], preferred_element_type=f32)
        a_ref[:, cols] = r[:, :FGD].astype(bf16)
        b_ref[:, cols] = r[:, FGD:].astype(bf16)


def _four_s1_kernel(a_ref, b_ref, g_ref, yr_ref, yi_ref):
    u = jnp.concatenate([a_ref[...], b_ref[...]], axis=0)
    y = jnp.dot(g_ref[...], u, preferred_element_type=f32)
    yr_ref[...] = y[:FN1].astype(bf16)
    yi_ref[...] = y[FN1:].astype(bf16)


def _four_s2_kernel(yr_ref, yi_ref, ct_ref, st_ref, t_ref, o_ref):
    yr = yr_ref[0].astype(f32)
    yi = yi_ref[0].astype(f32)
    ct = ct_ref[0]
    st = st_ref[0]
    v = jnp.concatenate([(yr * ct + yi * st).astype(bf16), (yi * ct - yr * st).astype(bf16)], axis=0)
    o_ref[...] = (jnp.dot(t_ref[...], v, preferred_element_type=f32) * (SEQ * FGD) ** -0.5).astype(bf16)


def _four_ctx_kernel(a_ref, b_ref, t_ref, o_ref):
    u = jnp.concatenate([a_ref[...], b_ref[...]], axis=0)
    o_ref[...] = (jnp.dot(t_ref[...], u, preferred_element_type=f32) * (NC * FGD) ** -0.5).astype(bf16)


def _fourier(p, tabs, with_ctx):
    a, b = pl.pallas_call(
        _four_ch_kernel,
        grid=(NT,),
        in_specs=[pl.BlockSpec((TM, 1024), lambda i: (i, C_FOUR // 1024)),
                  pl.BlockSpec((FGD, 2 * FGD), lambda i: (0, 0))],
        out_specs=[pl.BlockSpec((TM, 1024), lambda i: (i, 0))] * 2,
        out_shape=[jax.ShapeDtypeStruct((N, 1024), bf16)] * 2,
        compiler_params=_cparams(("arbitrary",)),
        name="four_ch",
    )(p, tabs["tch"])
    w = FN2 * 1024
    tn = 4096
    yr, yi = pl.pallas_call(
        _four_s1_kernel,
        grid=(w // tn,),
        in_specs=[pl.BlockSpec((FN1, tn), lambda j: (0, j)),
                  pl.BlockSpec((FN1, tn), lambda j: (0, j)),
                  pl.BlockSpec((2 * FN1, 2 * FN1), lambda j: (0, 0))],
        out_specs=[pl.BlockSpec((FN1, tn), lambda j: (0, j))] * 2,
        out_shape=[jax.ShapeDtypeStruct((FN1, w), bf16)] * 2,
        compiler_params=_cparams(("arbitrary",)),
        name="four_s1",
    )(a[NC:].reshape(FN1, w), b[NC:].reshape(FN1, w), tabs["g1"])
    o2 = pl.pallas_call(
        _four_s2_kernel,
        grid=(FN1,),
        in_specs=[pl.BlockSpec((1, FN2, 1024), lambda k: (k, 0, 0)),
                  pl.BlockSpec((1, FN2, 1024), lambda k: (k, 0, 0)),
                  pl.BlockSpec((1, FN2, 1), lambda k: (k, 0, 0)),
                  pl.BlockSpec((1, FN2, 1), lambda k: (k, 0, 0)),
                  pl.BlockSpec((FN2, 2 * FN2), lambda k: (0, 0))],
        out_specs=pl.BlockSpec((FN2, 1024), lambda k: (0, k)),
        out_shape=jax.ShapeDtypeStruct((FN2, FN1 * 1024), bf16),
        compiler_params=_cparams(("arbitrary",)),
        name="four_s2",
    )(yr.reshape(FN1, FN2, 1024), yi.reshape(FN1, FN2, 1024), tabs["ct"], tabs["st"], tabs["t2"])
    f_lat = o2.reshape(SEQ, 1024)
    if with_ctx:
        f_ctx = pl.pallas_call(
            _four_ctx_kernel,
            grid=(1,),
            in_specs=[pl.BlockSpec((NC, 1024), lambda i: (0, 0)),
                      pl.BlockSpec((NC, 1024), lambda i: (0, 0)),
                      pl.BlockSpec((NC, 2 * NC), lambda i: (0, 0))],
            out_specs=pl.BlockSpec((NC, 1024), lambda i: (0, 0)),
            out_shape=jax.ShapeDtypeStruct((NC, 1024), bf16),
            name="four_ctx",
        )(a, b, tabs["tctx"])
    else:
        f_ctx = jnp.zeros((NC, 1024), bf16)
    return jnp.concatenate([f_ctx, f_lat], axis=0)


def _outproj_kernel(m0, m1, m2, m3, w_ref, x_ref, g_ref, o_ref):
    m = pl.program_id(0)
    acc = jnp.dot(m0[...], w_ref[0:1024, :], preferred_element_type=f32)
    acc += jnp.dot(m1[...], w_ref[1024:2048, :], preferred_element_type=f32)
    acc += jnp.dot(m2[...], w_ref[2048:3072, :], preferred_element_type=f32)
    acc += jnp.dot(m3[...], w_ref[3072:4096, :], preferred_element_type=f32)
    row = m * TM_MM + lax.broadcasted_iota(i32, (TM_MM, 1), 0)
    gate = jnp.where(row < NC, g_ref[1:2, :], g_ref[0:1, :])
    o_ref[...] = x_ref[...] + gate * acc


def _outproj(mix, w, x, mod):
    tn = 512
    mspec = pl.BlockSpec((TM_MM, 1024), lambda m, n: (m, 0))
    return pl.pallas_call(
        _outproj_kernel,
        grid=(N // TM_MM, D // tn),
        in_specs=[mspec, mspec, mspec, mspec,
                  pl.BlockSpec((D, tn), lambda m, n: (0, n)),
                  pl.BlockSpec((TM_MM, tn), lambda m, n: (m, n)),
                  pl.BlockSpec((2, tn), lambda m, n: (0, 2 * (D // tn) + n))],
        out_specs=pl.BlockSpec((TM_MM, tn), lambda m, n: (m, n)),
        out_shape=jax.ShapeDtypeStruct((N, D), f32),
        compiler_params=_cparams(("arbitrary", "arbitrary"), VMEM_BIG),
        name="outproj",
    )(*mix, w, x, mod)


def _router_kernel(x_ref, g_ref, sh_ref, sc_ref, rw_ref, rb_ref, tri_ref,
                   h_out, ti_out, tw_out, pos_out, cnt_out, carry, hhi, hlo):
    i = pl.program_id(0)

    @pl.when(i == 0)
    def _():
        carry[...] = jnp.zeros_like(carry)

    is_ctx = i == 0
    sc = jnp.where(is_ctx, sc_ref[1:2, :], sc_ref[0:1, :])
    sh = jnp.where(is_ctx, sh_ref[1:2, :], sh_ref[0:1, :])

    def chunk(r0):
        rows = pl.ds(r0, RC)
        h = _rms(x_ref[rows, :]) * g_ref[...] * (1.0 + sc) + sh
        h_out[rows] = h.reshape(RC, 1, D)
        h_hi = h.astype(bf16)
        hhi[rows, :] = h_hi
        hlo[rows, :] = (h - h_hi.astype(f32)).astype(bf16)

    _for_rows(TM, chunk)
    w = rw_ref[...]
    w_hi = w.astype(bf16)
    w_lo = (w - w_hi.astype(f32)).astype(bf16)
    logits = (jnp.dot(hhi[...], w_hi, preferred_element_type=f32)
              + jnp.dot(hlo[...], w_hi, preferred_element_type=f32)
              + jnp.dot(hhi[...], w_lo, preferred_element_type=f32)) + rb_ref[...]

    lane = lax.broadcasted_iota(i32, (TM, LANES), 1)
    cur = logits
    vals, idxs = [], []
    for _ in range(TOP_K):
        mx = jnp.max(cur, axis=-1, keepdims=True)
        idx = jnp.min(jnp.where(cur == mx, lane, LANES), axis=-1, keepdims=True)
        vals.append(mx)
        idxs.append(idx)
        cur = jnp.where(lane == idx, -jnp.inf, cur)
    es = [jnp.exp(v - vals[0]) for v in vals]
    den = es[0] + es[1] + es[2] + es[3]

    sel = jnp.zeros((TM, LANES), f32)
    for idx in idxs:
        sel = sel + (lane == idx).astype(f32)
    rank = carry[...] + jnp.dot(tri_ref[...], sel.astype(bf16), preferred_element_type=f32)
    carry[...] = carry[...] + jnp.sum(sel, axis=0, keepdims=True)

    ti = jnp.zeros((TM, LANES), i32)
    tw = jnp.zeros((TM, LANES), f32)
    ps = jnp.zeros((TM, LANES), i32)
    for k in range(TOP_K):
        at_k = lane == k
        pk = jnp.sum(jnp.where(lane == idxs[k], rank, 0.0), axis=-1, keepdims=True)
        ti = jnp.where(at_k, idxs[k], ti)
        tw = jnp.where(at_k, es[k] / den, tw)
        ps = jnp.where(at_k, pk.astype(i32), ps)
    ti_out[...] = ti
    tw_out[...] = tw
    pos_out[...] = ps
    cnt_out[...] = jnp.broadcast_to(carry[...], (8, LANES)).astype(i32)


def _router(x, g, mod, rw, rb, tri):
    const = lambda i: (0, 0)
    row = lambda i: (i, 0)
    return pl.pallas_call(
        _router_kernel,
        grid=(NT,),
        in_specs=[pl.BlockSpec((TM, D), row),
                  pl.BlockSpec((1, D), const),
                  pl.BlockSpec((2, D), lambda i: (0, 3)),
                  pl.BlockSpec((2, D), lambda i: (0, 4)),
                  pl.BlockSpec((D, LANES), const),
                  pl.BlockSpec((1, LANES), const),
                  pl.BlockSpec((TM, TM), const)],
        out_specs=[pl.BlockSpec((TM, 1, D), lambda i: (i, 0, 0)),
                   pl.BlockSpec((TM, LANES), row),
                   pl.BlockSpec((TM, LANES), row),
                   pl.BlockSpec((TM, LANES), row),
                   pl.BlockSpec((8, LANES), const)],
        out_shape=[jax.ShapeDtypeStruct((N, 1, D), f32),
                   jax.ShapeDtypeStruct((N, LANES), i32),
                   jax.ShapeDtypeStruct((N, LANES), f32),
                   jax.ShapeDtypeStruct((N, LANES), i32),
                   jax.ShapeDtypeStruct((8, LANES), i32)],
        scratch_shapes=[pltpu.VMEM((1, LANES), f32), pltpu.VMEM((TM, D), bf16), pltpu.VMEM((TM, D), bf16)],
        compiler_params=_cparams(("arbitrary",), VMEM_BIG),
        name="router",
    )(x, g, mod, mod, rw, rb, tri)


def _row_copy(src, dst, sem):
    return pltpu.make_async_copy(src, dst, sem)


def _scatter_kernel(dest_ref, h_ref, xs_hbm, sem):
    i = pl.program_id(0)

    def issue(r, c):
        for k in range(TOP_K):
            d = dest_ref[(i * TM + r) * TOP_K + k]
            _row_copy(h_ref.at[r], xs_hbm.at[d], sem).start()
        return c

    lax.fori_loop(0, TM, issue, 0)

    def drain(r, c):
        for k in range(TOP_K):
            _row_copy(h_ref.at[0], xs_hbm.at[0], sem).wait()
        return c

    lax.fori_loop(0, TM, drain, 0)


def _scatter(dest, h3):
    return pl.pallas_call(
        _scatter_kernel,
        grid_spec=pltpu.PrefetchScalarGridSpec(
            num_scalar_prefetch=1,
            grid=(NT,),
            in_specs=[pl.BlockSpec((TM, 1, D), lambda i, d: (i, 0, 0))],
            out_specs=pl.BlockSpec(memory_space=pl.ANY),
            scratch_shapes=[pltpu.SemaphoreType.DMA],
        ),
        out_shape=jax.ShapeDtypeStruct((N_SLOT, 1, D), f32),
        compiler_params=_cparams(("arbitrary",)),
        name="moe_scatter",
    )(dest, h3)


def _moe_up_kernel(st_ref, se_ref, lo_ref, ok_ref, xs_ref, w1_ref, b1_ref, act_ref, x2d):
    s = pl.program_id(0)
    lo = lo_ref[s]

    @pl.when(ok_ref[s] == 1)
    def _():
        x2d[...] = xs_ref[...].reshape(TM, D)
        z = jnp.zeros((TM, 2 * D_FF), f32) + b1_ref[0]
        kc = 1024
        for c in range(D // kc):
            z += jnp.dot(x2d[:, c * kc:(c + 1) * kc].astype(bf16),
                         w1_ref[0, c * kc:(c + 1) * kc, :].astype(bf16), preferred_element_type=f32)
        glu = jnp.minimum(z[:, :D_FF], LIMIT)
        lin = jnp.clip(z[:, D_FF:], -LIMIT, LIMIT)
        act = (glu * jax.nn.sigmoid(ALPHA * glu) * (lin + 1.0)).astype(bf16)

        @pl.when(lo == 0)
        def _():
            act_ref[...] = act

        @pl.when(lo > 0)
        def _():
            row = lax.broadcasted_iota(i32, (TM, D_FF), 0)
            act_ref[...] = jnp.where(row >= lo, act, act_ref[...])


def _moe_down_kernel(st_ref, se_ref, lo_ref, ok_ref, act_ref, w2_ref, b2_ref, o_ref, ybuf):
    s = pl.program_id(0)
    lo = lo_ref[s]

    @pl.when(ok_ref[s] == 1)
    def _():
        y = jnp.dot(act_ref[...], w2_ref[0].astype(bf16), preferred_element_type=f32) + b2_ref[0]

        @pl.when(lo == 0)
        def _():
            ybuf[...] = y

        @pl.when(lo > 0)
        def _():
            row = lax.broadcasted_iota(i32, (TM, D), 0)
            ybuf[...] = jnp.where(row >= lo, y, ybuf[...])

        o_ref[...] = ybuf[...].reshape(TM, 1, D)


def _moe_ffn(plan, xs, w1, b1, w2, b2, l):
    tile3 = lambda s, st, se, lo, ok: (st[s], 0, 0)
    tile2 = lambda s, st, se, lo, ok: (st[s], 0)
    wexp = lambda s, st, se, lo, ok: (l, se[s], 0, 0)
    act = pl.pallas_call(
        _moe_up_kernel,
        grid_spec=pltpu.PrefetchScalarGridSpec(
            num_scalar_prefetch=4,
            grid=(N_STEP,),
            in_specs=[pl.BlockSpec((TM, 1, D), tile3),
                      pl.BlockSpec((None, 1, D, 2 * D_FF), wexp),
                      pl.BlockSpec((None, 1, 1, 2 * D_FF), wexp)],
            out_specs=pl.BlockSpec((TM, D_FF), tile2),
            scratch_shapes=[pltpu.VMEM((TM, D), f32)],
        ),
        out_shape=jax.ShapeDtypeStruct((N_SLOT, D_FF), bf16),
        compiler_params=_cparams(("arbitrary",), VMEM_BIG),
        name="moe_up",
    )(*plan, xs, w1, b1.reshape(DEPTH, N_EXP, 1, 2 * D_FF))
    return pl.pallas_call(
        _moe_down_kernel,
        grid_spec=pltpu.PrefetchScalarGridSpec(
            num_scalar_prefetch=4,
            grid=(N_STEP,),
            in_specs=[pl.BlockSpec((TM, D_FF), tile2),
                      pl.BlockSpec((None, 1, D_FF, D), wexp),
                      pl.BlockSpec((None, 1, 1, D), wexp)],
            out_specs=pl.BlockSpec((TM, 1, D), tile3),
            scratch_shapes=[pltpu.VMEM((TM, D), f32)],
        ),
        out_shape=jax.ShapeDtypeStruct((N_SLOT, 1, D), f32),
        compiler_params=_cparams(("arbitrary",), VMEM_BIG),
        name="moe_down",
    )(*plan, act, w2, b2.reshape(DEPTH, N_EXP, 1, D))


def _combine_kernel(dest_ref, x_ref, tw_ref, g_ref, fg_ref, ys_hbm, o_ref, gb0, gb1, gb2, gb3, t2d, sem,
                    *, first_tile, final):
    i = pl.program_id(0) + first_tile
    gbs = (gb0, gb1, gb2, gb3)

    def issue(r, c):
        for k in range(TOP_K):
            d = dest_ref[(i * TM + r) * TOP_K + k]
            _row_copy(ys_hbm.at[d], gbs[k].at[r], sem).start()
        return c

    lax.fori_loop(0, TM, issue, 0)

    def drain(r, c):
        for k in range(TOP_K):
            _row_copy(ys_hbm.at[0], gbs[k].at[0], sem).wait()
        return c

    lax.fori_loop(0, TM, drain, 0)

    gate = jnp.where(i == 0, g_ref[1:2, :], g_ref[0:1, :])

    def chunk(r0):
        rows = pl.ds(r0, RC)
        y = jnp.zeros((RC, D), f32)
        for k in range(TOP_K):
            t2d[k] = gbs[k][rows].reshape(RC, D)
            y = y + tw_ref[rows, k:k + 1] * t2d[k]
        xo = x_ref[rows, :] + gate * y
        if final:
            xo = _rms(xo) * fg_ref[...]
        o_ref[rows, :] = xo

    _for_rows(TM, chunk)


def _combine(dest, x, tw, mod, fg, ys, final):
    first = 1 if final else 0
    nt = NT - first
    row = lambda i, d: (i + first, 0)
    return pl.pallas_call(
        functools.partial(_combine_kernel, first_tile=first, final=final),
        grid_spec=pltpu.PrefetchScalarGridSpec(
            num_scalar_prefetch=1,
            grid=(nt,),
            in_specs=[pl.BlockSpec((TM, D), row),
                      pl.BlockSpec((TM, LANES), row),
                      pl.BlockSpec((2, D), lambda i, d: (0, 5)),
                      pl.BlockSpec((1, D), lambda i, d: (0, 0)),
                      pl.BlockSpec(memory_space=pl.ANY)],
            out_specs=pl.BlockSpec((TM, D), lambda i, d: (i, 0)),
            scratch_shapes=[pltpu.VMEM((TM, 1, D), f32)] * 4
            + [pltpu.VMEM((TOP_K, RC, D), f32), pltpu.SemaphoreType.DMA],
        ),
        out_shape=jax.ShapeDtypeStruct((nt * TM, D), f32),
        compiler_params=_cparams(("arbitrary",), VMEM_BIG),
        name="moe_combine",
    )(dest, x, tw, mod, fg, ys)


def _rope_tables():
    t = jnp.arange(SEQ, dtype=i32)
    r = (t // GRID_W).astype(f32)
    col = (t % GRID_W).astype(f32)
    per_axis = QK_ROPE // 4
    inv = ROPE_THETA ** (-jnp.arange(per_axis, dtype=f32) / per_axis)
    ang = jnp.concatenate([r[:, None] * inv, col[:, None] * inv], axis=-1)
    cos, sin = jnp.cos(ang), jnp.sin(ang)
    z32 = jnp.zeros((SEQ, 32), f32)
    z64 = jnp.zeros((SEQ, 64), f32)
    lat = [jnp.concatenate([cos, cos, z64], 1),
           jnp.concatenate([-sin, z32, z64], 1),
           jnp.concatenate([z32, sin, z64], 1)]
    ctx_cos = jnp.concatenate([jnp.ones((NC, 64), f32), jnp.zeros((NC, 64), f32)], 1)
    ctx_zero = jnp.zeros((NC, LANES), f32)
    return (jnp.concatenate([ctx_cos, lat[0]], 0),
            jnp.concatenate([ctx_zero, lat[1]], 0),
            jnp.concatenate([ctx_zero, lat[2]], 0))


def _routing_plan(top_i, pos, cnt):
    cnt = cnt[0, :N_EXP]
    ends = jnp.cumsum(cnt).astype(i32)
    off = ends - cnt
    ti = top_i[:, :TOP_K]
    onehot = (ti[:, :, None] == jnp.arange(N_EXP, dtype=i32)[None, None, :]).astype(i32)
    dest = (jnp.sum(onehot * off[None, None, :], axis=-1) + pos[:, :TOP_K]).astype(i32).reshape(-1)
    n_tile = N_SLOT // TM
    bp = jnp.sort(jnp.concatenate([jnp.arange(1, n_tile + 1, dtype=i32) * TM, ends]))
    seg_lo = jnp.concatenate([jnp.zeros((1,), i32), bp[:-1]])
    ok = (bp > seg_lo).astype(i32)
    st = jnp.minimum(seg_lo // TM, n_tile - 1)
    se = jnp.minimum(jnp.sum((ends[None, :] <= seg_lo[:, None]).astype(i32), axis=1), N_EXP - 1).astype(i32)
    return dest, (st, se, seg_lo - st * TM, ok)


def kernel(x, c, ctx, c_ctx, norm1_g, norm2_g, w_ada, b_ada, w_in, q_norm_g, w_uq, kv_norm_g, w_ukv, conv_w, conv_b, conv_norm_g, gmlp_v_norm_g, gmlp_ws, gmlp_bs, w_out, router_w, router_b, exp_w1, exp_b1, exp_w2, exp_b2, final_norm_g):
    xs = jnp.concatenate([ctx[0], x[0]], axis=0)
    cc = jnp.concatenate([c, c_ctx[None, :], c, c_ctx[None, :], jnp.zeros((4, D), f32)], axis=0)
    cosf, sina, sinb = _rope_tables()
    tabs = _fourier_tables()
    tri = (jnp.arange(TM)[:, None] > jnp.arange(TM)[None, :]).astype(bf16)
    fg = final_norm_g.reshape(1, D)

    for l in range(DEPTH):
        last = l == DEPTH - 1
        mod = _ada(cc, w_ada, b_ada, l)

        wi = w_in[l]
        w_in_p = jnp.concatenate(
            [wi[:, 1600:5696], wi[:, 5696:6720], wi[:, 576:1600], wi[:, :576],
             jnp.zeros((D, P_COLS - 6720), f32)], axis=1).astype(bf16)
        p = _inproj(xs, norm1_g[l].reshape(1, D), mod, w_in_p)

        wq = jnp.pad(w_uq[l], ((0, 0), (0, 0), (0, QK_PAD - QK_NOPE - QK_ROPE))).reshape(Q_LORA, HEADS * QK_PAD).astype(bf16)
        wkn = w_ukv[l][:, :, :QK_NOPE].reshape(KV_LORA, HEADS * QK_NOPE).astype(bf16)
        wv = w_ukv[l][:, :, QK_NOPE:].reshape(KV_LORA, HEADS * V_HEAD).astype(bf16)
        q, k, v = _mla_prep(p, wq, wkn, wv, q_norm_g[l].reshape(1, Q_LORA), kv_norm_g[l].reshape(1, KV_LORA),
                            cosf, sina, sinb)
        o_mla = _attention(q, k, v)

        cw = jnp.pad(conv_w[l], ((0, 32 - CONV_W), (0, 0)))
        a_out = _conv(p, cw, conv_b[l].reshape(1, 1024), conv_norm_g[l].reshape(1, 1024))
        bsb = jnp.broadcast_to(gmlp_bs[l][:, :, None], (HEADS, CHUNK, LANES))
        g_out = _gmlp(p, gmlp_ws[l].astype(bf16), bsb, gmlp_v_norm_g[l])
        f_out = _fourier(p, tabs, with_ctx=not last)

        xs = _outproj((o_mla, a_out, g_out, f_out), w_out[l].astype(bf16), xs, mod)

        rw = jnp.pad(router_w[l], ((0, 0), (0, LANES - N_EXP)))
        rb = jnp.concatenate([router_b[l], jnp.full((LANES - N_EXP,), -1e30, f32)]).reshape(1, LANES)
        h3, top_i, top_w, pos, cnt = _router(xs, norm2_g[l].reshape(1, D), mod, rw, rb, tri)
        dest, plan = _routing_plan(top_i, pos, cnt)
        xsort = _scatter(dest, h3)
        ys = _moe_ffn(plan, xsort, exp_w1, exp_b1, exp_w2, exp_b2, l)
        xs = _combine(dest, xs, top_w, mod, fg, ys, final=last)

    return xs.reshape(1, SEQ, D)
```
